```python
import math
import jax
import jax.numpy as jnp
from jax import lax
import numpy as np

D_MODEL = 1024
BATCH = 8
SEQ = 4096
DEPTH = 2
DEC_BATCH = 128
DEC_SEQ = 8
PAST_LEN = 16384
PAGE_SIZE = 128

N_HEADS = 16
HEAD_DIM = D_MODEL // N_HEADS
KV_HEADS = 2
GROUP = N_HEADS // KV_HEADS
IDX_HEADS = 8
IDX_DIM = 64
TOPK_MAX = 256
WINDOW = 128
N_BUCKETS = 32
MAX_DISTANCE = 128
D_FF = 2688
RMS_EPS = 1e-6
Q_BLOCK = 128
N_A_LAYERS = (DEPTH + 1) // 2
N_B_LAYERS = DEPTH // 2
QW = N_HEADS * HEAD_DIM
KVW = KV_HEADS * HEAD_DIM
A_SPLITS = (QW, QW + KVW, QW + 2 * KVW, QW + 2 * KVW + IDX_HEADS * IDX_DIM,
            QW + 2 * KVW + IDX_HEADS * IDX_DIM + IDX_DIM)
A_IN = A_SPLITS[-1] + IDX_HEADS
B_SPLITS = (QW, QW + KVW)
B_IN = QW + 2 * KVW
NEG = float(np.finfo(np.float32).min)

kernel_name = 'hybrid_dsa_swa_macaron_step'


def rmsnorm(x, g):
    xf = x.astype(jnp.float32)
    y = xf * lax.rsqrt(jnp.mean(xf * xf, axis=-1, keepdims=True) + RMS_EPS)
    return (y * g.astype(jnp.float32)).astype(x.dtype)


def swiglu(h, w_gate, w_up, w_down):
    return (jax.nn.silu(h @ w_gate) * (h @ w_up)) @ w_down


def t5_bucket(dist):
    n = jnp.maximum(dist, 0)
    max_exact = N_BUCKETS // 2
    nf = jnp.maximum(n, 1).astype(jnp.float32)
    log_part = jnp.log(nf / max_exact) / math.log(MAX_DISTANCE / max_exact) * (N_BUCKETS - max_exact)
    large = jnp.minimum(max_exact + log_part.astype(jnp.int32), N_BUCKETS - 1)
    return jnp.where(n < max_exact, n, large)


def rel_bias(dist, table):
    return table[t5_bucket(dist)].astype(jnp.float32)


def sink_softmax(logits, sinks):
    s = sinks.astype(jnp.float32).reshape(KV_HEADS, GROUP, 1, 1)
    m = jnp.maximum(jnp.max(logits, axis=-1, keepdims=True), s)
    e = jnp.exp(logits - m)
    return e / (jnp.sum(e, axis=-1, keepdims=True) + jnp.exp(s - m))


def split_a(h, w_in):
    lead = h.shape[:-1]
    q, k, v, qi, ki, wi = jnp.split(h @ w_in, A_SPLITS, axis=-1)
    return (q.reshape(*lead, KV_HEADS, GROUP, HEAD_DIM), k.reshape(*lead, KV_HEADS, HEAD_DIM),
            v.reshape(*lead, KV_HEADS, HEAD_DIM), qi.reshape(*lead, IDX_HEADS, IDX_DIM), ki, wi)


def split_b(h, w_in):
    lead = h.shape[:-1]
    q, k, v = jnp.split(h @ w_in, B_SPLITS, axis=-1)
    return (q.reshape(*lead, KV_HEADS, GROUP, HEAD_DIM), k.reshape(*lead, KV_HEADS, HEAD_DIM),
            v.reshape(*lead, KV_HEADS, HEAD_DIM))


def indexer_scores(qi, wi, ki):
    dots = jnp.einsum('bthd,bsd->bths', qi, ki).astype(jnp.float32) * IDX_DIM ** -0.5
    return jnp.einsum('bths,bth->bts', jax.nn.relu(dots), wi.astype(jnp.float32) * IDX_HEADS ** -0.5)


def sparse_attend(q, k_sel, v_sel, dist, table):
    bias = rel_bias(dist, table)
    bias = jnp.moveaxis(bias.reshape(*bias.shape[:-1], KV_HEADS, GROUP), -3, -1)
    logits = jnp.einsum('btngd,btjnd->btngj', q, k_sel).astype(jnp.float32) * HEAD_DIM ** -0.5 + bias
    logits = jnp.where((dist >= 0)[:, :, None, None, :], logits, NEG)
    p = jax.nn.softmax(logits, axis=-1).astype(v_sel.dtype)
    o = jnp.einsum('btngj,btjnd->btngd', p, v_sel)
    return o.reshape(*o.shape[:2], QW)


def dsa_prompt(h, w_in, w_out, table):
    b, t, _ = h.shape
    q, k, v, qi, ki, wi = split_a(h, w_in)
    topk = min(TOPK_MAX, t // 4)
    n_blk = t // Q_BLOCK
    key_pos = jnp.arange(t, dtype=jnp.int32)
    bidx = jnp.arange(b)[:, None, None]

    def to_blocks(a):
        return jnp.moveaxis(a.reshape(b, n_blk, Q_BLOCK, *a.shape[2:]), 1, 0)

    def block(args):
        qb, qib, wib, qpos = args
        scores = indexer_scores(qib, wib, ki)
        scores = jnp.where(key_pos[None, None, :] <= qpos[None, :, None], scores, NEG)
        _, sel = lax.top_k(scores, topk)
        dist = qpos[None, :, None] - sel
        return sparse_attend(qb, k[bidx, sel], v[bidx, sel], dist, table)

    out = lax.map(block, (to_blocks(q), to_blocks(qi), to_blocks(wi), key_pos.reshape(n_blk, Q_BLOCK)))
    out = jnp.moveaxis(out, 0, 1).reshape(b, t, QW)
    return out @ w_out, jnp.stack([k, v], axis=2), ki


def dsa_sample(h, cache_kv, cache_idx, layer, page_table, w_in, w_out, table):
    b, tn, _ = h.shape
    past = page_table.shape[1] * PAGE_SIZE
    topk = min(TOPK_MAX, (past + tn) // 4)
    q, k, v, qi, ki, wi = split_a(h, w_in)
    kv_new = jnp.stack([k, v], axis=2)
    ki_all = jnp.concatenate([cache_idx[layer, page_table].reshape(b, past, IDX_DIM), ki], axis=1)
    qpos = past + jnp.arange(tn, dtype=jnp.int32)
    key_pos = jnp.arange(past + tn, dtype=jnp.int32)
    scores = indexer_scores(qi, wi, ki_all)
    scores = jnp.where(key_pos[None, None, :] <= qpos[None, :, None], scores, NEG)
    _, sel = lax.top_k(scores, topk)
    bidx = jnp.arange(b)[:, None, None]
    sel_past = jnp.minimum(sel, past - 1)
    phys = page_table[bidx, sel_past // PAGE_SIZE]
    kv_past = cache_kv[layer, phys, sel_past % PAGE_SIZE]
    kv_cur = kv_new[bidx, jnp.clip(sel - past, 0, tn - 1)]
    kv_sel = jnp.where((sel < past)[..., None, None, None], kv_past, kv_cur)
    dist = qpos[None, :, None] - sel
    out = sparse_attend(q, kv_sel[..., 0, :, :], kv_sel[..., 1, :, :], dist, table)
    return out @ w_out, kv_new, ki


def swa_prompt(h, w_in, w_out, sinks, table):
    b, t, _ = h.shape
    n_blk = t // WINDOW
    q, k, v = split_b(h, w_in)
    qb = q.reshape(b, n_blk, WINDOW, KV_HEADS, GROUP, HEAD_DIM)

    def band(a):
        ab = a.reshape(b, n_blk, WINDOW, KV_HEADS, HEAD_DIM)
        prev = jnp.concatenate([jnp.zeros_like(ab[:, :1]), ab[:, :-1]], axis=1)
        return jnp.concatenate([prev, ab], axis=2)

    kb, vb = band(k), band(v)
    qloc = jnp.arange(WINDOW, dtype=jnp.int32)[:, None] + WINDOW
    kloc = jnp.arange(2 * WINDOW, dtype=jnp.int32)[None, :]
    dist = qloc - kloc
    valid = (dist >= 0) & (dist <= WINDOW)
    first = (jnp.arange(n_blk) == 0)[:, None, None]
    valid = valid[None] & ~(first & (kloc < WINDOW)[None])
    bias = jnp.transpose(rel_bias(dist, table).reshape(WINDOW, 2 * WINDOW, KV_HEADS, GROUP), (2, 3, 0, 1))
    logits = jnp.einsum('bcqngd,bcknd->bcngqk', qb, kb).astype(jnp.float32) * HEAD_DIM ** -0.5 + bias
    logits = jnp.where(valid[None, :, None, None], logits, NEG)
    p = sink_softmax(logits, sinks).astype(vb.dtype)
    o = jnp.einsum('bcngqk,bcknd->bcqngd', p, vb).reshape(b, t, QW)
    kv = jnp.stack([k, v], axis=2)
    return o @ w_out, kv[:, t - WINDOW:]


def swa_sample(h, state, past, w_in, w_out, sinks, table):
    b, tn, _ = h.shape
    q, k, v = split_b(h, w_in)
    kv_all = jnp.concatenate([state, jnp.stack([k, v], axis=2)], axis=1)
    qpos = past + jnp.arange(tn, dtype=jnp.int32)
    kpos = past - WINDOW + jnp.arange(WINDOW + tn, dtype=jnp.int32)
    dist = qpos[:, None] - kpos[None, :]
    valid = (dist >= 0) & (dist <= WINDOW)
    bias = jnp.transpose(rel_bias(dist, table).reshape(tn, WINDOW + tn, KV_HEADS, GROUP), (2, 3, 0, 1))
    logits = jnp.einsum('bqngd,bknd->bngqk', q, kv_all[:, :, 0]).astype(jnp.float32) * HEAD_DIM ** -0.5 + bias
    logits = jnp.where(valid, logits, NEG)
    p = sink_softmax(logits, sinks).astype(kv_all.dtype)
    o = jnp.einsum('bngqk,bknd->bqngd', p, kv_all[:, :, 1]).reshape(b, tn, QW)
    return o @ w_out, kv_all[:, tn:]


def setup_inputs(seed: int = 0) -> dict:
    key = jax.random.key(seed)
    ks = jax.random.split(key, 18)
    n_pages = PAST_LEN // PAGE_SIZE
    n_pool = (5 * DEC_BATCH * n_pages) // 4

    def nrm(k, shape, scale=1.0):
        return jax.random.normal(k, shape, jnp.float32) * scale

    page_table = jax.random.permutation(ks[5], n_pool)[:DEC_BATCH * n_pages].reshape(DEC_BATCH, n_pages).astype(jnp.int32)
    return {
        'x_prompt': nrm(ks[0], (BATCH, SEQ, D_MODEL)),
        'x_sample': nrm(ks[1], (DEC_BATCH, DEC_SEQ, D_MODEL)),
        'cache_kv': nrm(ks[2], (N_A_LAYERS, n_pool, PAGE_SIZE, 2, KV_HEADS, HEAD_DIM)),
        'cache_idx': nrm(ks[3], (N_A_LAYERS, n_pool, PAGE_SIZE, IDX_DIM)),
        'state_win_kv': nrm(ks[4], (N_B_LAYERS, DEC_BATCH, WINDOW, 2, KV_HEADS, HEAD_DIM)),
        'page_table': page_table,
        'rel_bias': nrm(ks[6], (N_BUCKETS, N_HEADS), 0.5),
        'ffn_norm': 1.0 + nrm(ks[7], (DEPTH, 2, D_MODEL), 0.1),
        'ffn_w_gate': nrm(ks[8], (DEPTH, 2, D_MODEL, D_FF), D_MODEL ** -0.5),
        'ffn_w_up': nrm(ks[9], (DEPTH, 2, D_MODEL, D_FF), D_MODEL ** -0.5),
        'ffn_w_down': nrm(ks[10], (DEPTH, 2, D_FF, D_MODEL), D_FF ** -0.5),
        'mix_norm': 1.0 + nrm(ks[11], (DEPTH, D_MODEL), 0.1),
        'a_w_in': nrm(ks[12], (N_A_LAYERS, D_MODEL, A_IN), D_MODEL ** -0.5),
        'a_w_out': nrm(ks[13], (N_A_LAYERS, QW, D_MODEL), QW ** -0.5),
        'b_w_in': nrm(ks[14], (N_B_LAYERS, D_MODEL, B_IN), D_MODEL ** -0.5),
        'b_w_out': nrm(ks[15], (N_B_LAYERS, QW, D_MODEL), QW ** -0.5),
        'b_sinks': nrm(ks[16], (N_B_LAYERS, N_HEADS)),
        'final_norm': 1.0 + nrm(ks[17], (D_MODEL,), 0.1),
    }


def reference(x_prompt, x_sample, cache_kv, cache_idx, state_win_kv, page_table, rel_bias, ffn_norm,
              ffn_w_gate, ffn_w_up, ffn_w_down, mix_norm, a_w_in, a_w_out, b_w_in, b_w_out, b_sinks, final_norm):
    past = page_table.shape[1] * PAGE_SIZE
    xp, xs = x_prompt, x_sample
    kv_p, idx_p, win_p, kv_s, idx_s, win_s = [], [], [], [], [], []
    for i in range(DEPTH):
        xp = xp + 0.5 * swiglu(rmsnorm(xp, ffn_norm[i, 0]), ffn_w_gate[i, 0], ffn_w_up[i, 0], ffn_w_down[i, 0])
        xs = xs + 0.5 * swiglu(rmsnorm(xs, ffn_norm[i, 0]), ffn_w_gate[i, 0], ffn_w_up[i, 0], ffn_w_down[i, 0])
        hp = rmsnorm(xp, mix_norm[i])
        hs = rmsnorm(xs, mix_norm[i])
        j = i // 2
        if i % 2 == 0:
            yp, kvp, kip = dsa_prompt(hp, a_w_in[j], a_w_out[j], rel_bias)
            ys, kvs, kis = dsa_sample(hs, cache_kv, cache_idx, j, page_table, a_w_in[j], a_w_out[j], rel_bias)
            kv_p.append(kvp)
            idx_p.append(kip)
            kv_s.append(kvs)
            idx_s.append(kis)
        else:
            yp, wp = swa_prompt(hp, b_w_in[j], b_w_out[j], b_sinks[j], rel_bias)
            ys, ws = swa_sample(hs, state_win_kv[j], past, b_w_in[j], b_w_out[j], b_sinks[j], rel_bias)
            win_p.append(wp)
            win_s.append(ws)
        xp = xp + yp
        xs = xs + ys
        xp = xp + 0.5 * swiglu(rmsnorm(xp, ffn_norm[i, 1]), ffn_w_gate[i, 1], ffn_w_up[i, 1], ffn_w_down[i, 1])
        xs = xs + 0.5 * swiglu(rmsnorm(xs, ffn_norm[i, 1]), ffn_w_gate[i, 1], ffn_w_up[i, 1], ffn_w_down[i, 1])
    y_prompt = rmsnorm(xp, final_norm)
    y_sample = rmsnorm(xs, final_norm)
    return (y_prompt, y_sample, jnp.stack(kv_p), jnp.stack(idx_p), jnp.stack(win_p),
            jnp.stack(kv_s), jnp.stack(idx_s), jnp.stack(win_s))
```

```python
import functools
import math

import jax
import jax.numpy as jnp
import numpy as np
from jax import lax
from jax.experimental import pallas as pl
from jax.experimental.pallas import tpu as pltpu

F32 = jnp.float32
BF16 = jnp.bfloat16
I32 = jnp.int32

RMS_EPS = 1e-6
NEG = float(np.finfo(np.float32).min)
LANES = 128
PAGE_SIZE = 128
WINDOW = 128
N_HEADS = 16
KV_HEADS = 2
GROUP = N_HEADS // KV_HEADS
HEAD_DIM = 64
IDX_HEADS = 8
IDX_DIM = 64
TOPK_MAX = 256
N_BUCKETS = 32
MAX_DISTANCE = 128
QW = N_HEADS * HEAD_DIM
KVW = KV_HEADS * HEAD_DIM
A_USED = QW + 2 * KVW + IDX_HEADS * IDX_DIM + IDX_DIM + IDX_HEADS
A_PAD = 1920
QI_OFF = QW + 2 * KVW
KIW_OFF = QI_OFF + IDX_HEADS * IDX_DIM
KEY_NEG_INF = -2139095041
KEY_POS_INF = 2139095040
SEARCH_STEPS = 32
VMEM_LIMIT = 56 * 1024 * 1024

TOKEN_TILE = 512
KEY_CHUNK = 512
BLOCKS_PER_CHUNK = KEY_CHUNK // LANES
Q_TILE = 128
KV_GROUP = 8
SWA_BATCH = 8


def _rms(x, g):
    return x * lax.rsqrt(jnp.mean(x * x, axis=-1, keepdims=True) + RMS_EPS) * g


def _params(*sem):
    return pltpu.CompilerParams(dimension_semantics=sem, vmem_limit_bytes=VMEM_LIMIT)


def _resident(shape, index_map):
    return pl.BlockSpec(shape, index_map, pipeline_mode=pl.Buffered(1))


def _ffn_body(x_ref, g_ref, wg_ref, wu_ref, wd_ref, gf_ref, o_ref, *, ff_chunk, final):
    x = x_ref[...]
    h = _rms(x, g_ref[...]).astype(BF16)
    acc = jnp.zeros(x.shape, F32)
    for c in range(wg_ref.shape[1] // ff_chunk):
        sl = pl.ds(c * ff_chunk, ff_chunk)
        gate = jnp.dot(h, wg_ref[:, sl], preferred_element_type=F32)
        up = jnp.dot(h, wu_ref[:, sl], preferred_element_type=F32)
        a = (jax.nn.silu(gate) * up).astype(BF16)
        acc = acc + jnp.dot(a, wd_ref[sl, :], preferred_element_type=F32)
    y = x + 0.5 * acc
    if final:
        y = _rms(y, gf_ref[...])
    o_ref[...] = y


def _ffn(x, g, wg, wu, wd, layer, half, gf, final):
    n, d = x.shape
    f = wg.shape[-1]
    body = functools.partial(_ffn_body, ff_chunk=f // 3, final=final)
    wspec = lambda r, c: _resident((None, None, r, c), lambda m: (layer, half, 0, 0))
    return pl.pallas_call(
        body,
        grid=(n // TOKEN_TILE,),
        in_specs=[
            pl.BlockSpec((TOKEN_TILE, d), lambda m: (m, 0)),
            _resident((None, 1, d), lambda m: (2 * layer + half, 0, 0)),
            wspec(d, f),
            wspec(d, f),
            wspec(f, d),
            _resident((1, d), lambda m: (0, 0)),
        ],
        out_specs=pl.BlockSpec((TOKEN_TILE, d), lambda m: (m, 0)),
        out_shape=jax.ShapeDtypeStruct((n, d), F32),
        compiler_params=_params("parallel"),
        name="ffn",
    )(x, g, wg, wu, wd, gf)


def _proj_a_body(x_ref, g_ref, w_ref, q_ref, kv_ref, qi_ref, kiw_ref):
    h = _rms(x_ref[...], g_ref[...]).astype(BF16)
    y = jnp.dot(h, w_ref[...], preferred_element_type=F32)
    q_ref[...] = (y[:, :QW] * 0.125).astype(BF16)
    kv_ref[...] = y[:, QW:QI_OFF]
    qi_ref[...] = (y[:, QI_OFF:KIW_OFF] * 0.125).astype(BF16)
    kiw_ref[...] = y[:, KIW_OFF:]


def _proj_a(x, g, w, layer):
    n, d = x.shape
    tile = lambda c: pl.BlockSpec((TOKEN_TILE, c), lambda m: (m, 0))
    return pl.pallas_call(
        _proj_a_body,
        grid=(n // TOKEN_TILE,),
        in_specs=[tile(d), _resident((None, 1, d), lambda m: (layer, 0, 0)), _resident((d, A_PAD), lambda m: (0, 0))],
        out_specs=[tile(QW), tile(2 * KVW), tile(IDX_HEADS * IDX_DIM), tile(LANES)],
        out_shape=[
            jax.ShapeDtypeStruct((n, QW), BF16),
            jax.ShapeDtypeStruct((n, 2 * KVW), F32),
            jax.ShapeDtypeStruct((n, IDX_HEADS * IDX_DIM), BF16),
            jax.ShapeDtypeStruct((n, LANES), F32),
        ],
        compiler_params=_params("parallel"),
        name="proj_a",
    )(x, g, w)


def _proj_b_body(x_ref, g_ref, w_ref, q_ref, kv_ref):
    h = _rms(x_ref[...], g_ref[...]).astype(BF16)
    y = jnp.dot(h, w_ref[...], preferred_element_type=F32)
    q_ref[...] = (y[:, :QW] * 0.125).astype(BF16)
    kv_ref[...] = y[:, QW:]


def _proj_b(x, g, w, layer):
    n, d = x.shape
    tile = lambda c: pl.BlockSpec((TOKEN_TILE, c), lambda m: (m, 0))
    return pl.pallas_call(
        _proj_b_body,
        grid=(n // TOKEN_TILE,),
        in_specs=[tile(d), _resident((None, 1, d), lambda m: (layer, 0, 0)), _resident((d, QW + 2 * KVW), lambda m: (0, 0))],
        out_specs=[tile(QW), tile(2 * KVW)],
        out_shape=[jax.ShapeDtypeStruct((n, QW), BF16), jax.ShapeDtypeStruct((n, 2 * KVW), F32)],
        compiler_params=_params("parallel"),
        name="proj_b",
    )(x, g, w)


def _proj_out_body(x_ref, o_ref, w_ref, y_ref):
    y_ref[...] = x_ref[...] + jnp.dot(o_ref[...], w_ref[...], preferred_element_type=F32)


def _proj_out(x, o, w):
    n, d = x.shape
    tile = lambda: pl.BlockSpec((TOKEN_TILE, d), lambda m: (m, 0))
    return pl.pallas_call(
        _proj_out_body,
        grid=(n // TOKEN_TILE,),
        in_specs=[tile(), pl.BlockSpec((TOKEN_TILE, QW), lambda m: (m, 0)), _resident((QW, d), lambda m: (0, 0))],
        out_specs=tile(),
        out_shape=jax.ShapeDtypeStruct((n, d), F32),
        compiler_params=_params("parallel"),
        name="proj_out",
    )(x, o, w)


def _key_to_float(key):
    bits = jnp.where(key >= 0, key, key ^ 0x7FFFFFFF)
    return lax.bitcast_convert_type(bits, F32)


def _count(s_ref, n_blocks, thr, strict, unroll):
    rows = s_ref.shape[1]
    tb = jnp.broadcast_to(thr, (rows, LANES))

    def body(c, acc):
        for j in range(unroll):
            s = s_ref[c * unroll + j]
            hit = (s > tb) if strict else (s >= tb)
            acc = acc + jnp.where(hit, 1.0, 0.0)
        return acc

    acc = lax.fori_loop(0, n_blocks // unroll, body, jnp.zeros((rows, LANES), F32))
    return jnp.sum(acc, axis=1, keepdims=True)


def _kth_largest(s_ref, n_blocks, k, unroll):
    rows = s_ref.shape[1]

    def step(_, carry):
        lo, hi = carry
        mid = (lo >> 1) + (hi >> 1) + (lo & hi & 1)
        ge = _count(s_ref, n_blocks, _key_to_float(mid), False, unroll) >= k
        return jnp.where(ge, mid, lo), jnp.where(ge, hi, mid)

    lo0 = jnp.full((rows, 1), KEY_NEG_INF, I32)
    hi0 = jnp.full((rows, 1), KEY_POS_INF, I32)
    lo, _ = lax.fori_loop(0, SEARCH_STEPS, step, (lo0, hi0))
    return _key_to_float(lo)


def _write_mask_bias(s_ref, mb_ref, tri_ref, n_blocks, k, unroll, valid_fn):
    rows = s_ref.shape[1]
    thr = _kth_largest(s_ref, n_blocks, k, unroll)
    need = k - _count(s_ref, n_blocks, thr, True, unroll)
    tb = jnp.broadcast_to(thr, (rows, LANES))
    nb = jnp.broadcast_to(need, (rows, LANES))

    def body(b, ties_before):
        s = s_ref[b]
        eq = s == tb
        pe = jnp.dot(eq.astype(BF16), tri_ref[...], preferred_element_type=F32)
        sel = (s > tb) | (eq & (ties_before + pe[:, :LANES] <= nb))
        mb_ref[b] = jnp.where(sel & valid_fn(b), 0.0, NEG)
        return ties_before + pe[:, LANES:]

    lax.fori_loop(0, n_blocks, body, jnp.zeros((rows, LANES), F32))


def _dsa_prompt_body(q_ref, qi_ref, kiw_ref, kt_ref, va_ref, kit_ref, bb_ref, tri_ref, o_ref,
                     s_ref, mb_ref, l_ref, m_ref, acc_ref, *, topk):
    tq = q_ref.shape[0]
    i = pl.program_id(1)
    n_chunks = ((i + 1) * tq + KEY_CHUNK - 1) // KEY_CHUNK
    n_blocks = n_chunks * BLOCKS_PER_CHUNK
    qpos = i * tq + lax.broadcasted_iota(I32, (tq, LANES), 0)
    lane = lax.broadcasted_iota(I32, (tq, LANES), 1)

    qi = qi_ref[...]
    w = kiw_ref[:, IDX_DIM:IDX_DIM + IDX_HEADS] * IDX_HEADS ** -0.5

    def score_chunk(c, _):
        kic = kit_ref[0, c]
        acc = jnp.zeros((tq, KEY_CHUNK), F32)
        for h in range(IDX_HEADS):
            d = jnp.dot(qi[:, h * IDX_DIM:(h + 1) * IDX_DIM], kic, preferred_element_type=F32)
            acc = acc + jnp.maximum(d, 0.0) * w[:, h:h + 1]
        for j in range(BLOCKS_PER_CHUNK):
            b = c * BLOCKS_PER_CHUNK + j
            s_ref[b] = jnp.where(b * LANES + lane <= qpos, acc[:, j * LANES:(j + 1) * LANES], NEG)
        return 0

    lax.fori_loop(0, n_chunks, score_chunk, 0)

    _write_mask_bias(s_ref, mb_ref, tri_ref, n_blocks, float(topk), BLOCKS_PER_CHUNK,
                     lambda b: b * LANES + lane <= qpos)

    q = q_ref[...]
    for n in range(KV_HEADS):
        qn = jnp.concatenate(
            [q[:, (n * GROUP + g) * HEAD_DIM:(n * GROUP + g + 1) * HEAD_DIM] for g in range(GROUP)], axis=0)

        def logits_chunk(c, _):
            lg = jnp.dot(qn, kt_ref[0, n, c], preferred_element_type=F32)
            for j in range(BLOCKS_PER_CHUNK):
                b = c * BLOCKS_PER_CHUNK + j
                mbj = mb_ref[b]
                for g in range(GROUP):
                    l_ref[b, g * tq:(g + 1) * tq, :] = lg[g * tq:(g + 1) * tq, j * LANES:(j + 1) * LANES] + mbj
            return 0

        lax.fori_loop(0, n_chunks, logits_chunk, 0)

        for g in range(GROUP):
            rows = pl.ds(g * tq, tq)
            l_ref[i, rows, :] = l_ref[i, rows, :] + bb_ref[n * GROUP + g, :, LANES:]

        @pl.when(i > 0)
        def _():
            for g in range(GROUP):
                rows = pl.ds(g * tq, tq)
                l_ref[i - 1, rows, :] = l_ref[i - 1, rows, :] + bb_ref[n * GROUP + g, :, :LANES]

        for g in range(GROUP):
            rows = pl.ds(g * tq, tq)

            def max_chunk(c, acc):
                for j in range(BLOCKS_PER_CHUNK):
                    acc = jnp.maximum(acc, l_ref[c * BLOCKS_PER_CHUNK + j, rows, :])
                return acc

            mx = lax.fori_loop(0, n_chunks, max_chunk, jnp.full((tq, LANES), NEG, F32))
            m_ref[rows, :] = jnp.broadcast_to(jnp.max(mx, axis=1, keepdims=True), (tq, LANES))

        acc_ref[...] = jnp.zeros(acc_ref.shape, F32)

        def pv_chunk(c, _):
            m = m_ref[...]
            p = jnp.concatenate(
                [jnp.exp(l_ref[c * BLOCKS_PER_CHUNK + j] - m).astype(BF16) for j in range(BLOCKS_PER_CHUNK)], axis=1)
            v = va_ref[0, n, pl.ds(pl.multiple_of(c * KEY_CHUNK, KEY_CHUNK), KEY_CHUNK), :]
            acc_ref[...] += jnp.dot(p, v, preferred_element_type=F32)
            return 0

        lax.fori_loop(0, n_chunks, pv_chunk, 0)

        acc = acc_ref[...]
        o = acc[:, :HEAD_DIM] * (1.0 / acc[:, HEAD_DIM:])
        for g in range(GROUP):
            h = n * GROUP + g
            o_ref[:, h * HEAD_DIM:(h + 1) * HEAD_DIM] = o[g * tq:(g + 1) * tq].astype(BF16)


def _dsa_prompt(q, qi, kiw, kt, va, kit, bb_rel, tri, batch, seq, topk):
    nq = seq // Q_TILE
    n_chunks = seq // KEY_CHUNK
    tile = lambda c: pl.BlockSpec((Q_TILE, c), lambda b, i: (b * nq + i, 0))
    return pl.pallas_call(
        functools.partial(_dsa_prompt_body, topk=topk),
        grid=(batch, nq),
        in_specs=[
            tile(QW), tile(IDX_HEADS * IDX_DIM), tile(LANES),
            pl.BlockSpec((1, KV_HEADS, n_chunks, HEAD_DIM, KEY_CHUNK), lambda b, i: (b, 0, 0, 0, 0)),
            pl.BlockSpec((1, KV_HEADS, seq, LANES), lambda b, i: (b, 0, 0, 0)),
            pl.BlockSpec((1, n_chunks, IDX_DIM, KEY_CHUNK), lambda b, i: (b, 0, 0, 0)),
            _resident((N_HEADS, Q_TILE, 2 * LANES), lambda b, i: (0, 0, 0)),
            _resident((LANES, 2 * LANES), lambda b, i: (0, 0)),
        ],
        out_specs=tile(QW),
        out_shape=jax.ShapeDtypeStruct((batch * seq, QW), BF16),
        scratch_shapes=[
            pltpu.VMEM((seq // LANES, Q_TILE, LANES), F32),
            pltpu.VMEM((seq // LANES, Q_TILE, LANES), F32),
            pltpu.VMEM((seq // LANES, GROUP * Q_TILE, LANES), F32),
            pltpu.VMEM((GROUP * Q_TILE, LANES), F32),
            pltpu.VMEM((GROUP * Q_TILE, LANES), F32),
        ],
        compiler_params=_params("parallel", "arbitrary"),
        name="dsa_prompt",
    )(q, qi, kiw, kt, va, kit, bb_rel, tri)


def _swa_prompt_body(sink_ref, q_ref, ktp_ref, kto_ref, vap_ref, vao_ref, bb_ref, o_ref):
    tq = q_ref.shape[0]
    c = pl.program_id(1)
    r = lax.broadcasted_iota(I32, (tq, 2 * WINDOW), 0)
    j = lax.broadcasted_iota(I32, (tq, 2 * WINDOW), 1)
    dist = r + WINDOW - j
    valid = (dist >= 0) & (dist <= WINDOW) & ((c > 0) | (j >= WINDOW))
    q = q_ref[...]
    for n in range(KV_HEADS):
        kt = jnp.concatenate([ktp_ref[0, n], kto_ref[0, n]], axis=1)
        va = jnp.concatenate([vap_ref[0, n], vao_ref[0, n]], axis=0)
        for g in range(GROUP):
            h = n * GROUP + g
            lg = jnp.dot(q[:, h * HEAD_DIM:(h + 1) * HEAD_DIM], kt, preferred_element_type=F32)
            lg = jnp.where(valid, lg + bb_ref[h], NEG)
            s = sink_ref[h]
            m = jnp.maximum(jnp.max(lg, axis=1, keepdims=True), s)
            e = jnp.exp(lg - m)
            inv = 1.0 / (jnp.sum(e, axis=1, keepdims=True) + jnp.exp(s - m))
            p = (e * inv).astype(BF16)
            o = jnp.dot(p, va, preferred_element_type=F32)
            o_ref[:, h * HEAD_DIM:(h + 1) * HEAD_DIM] = o[:, :HEAD_DIM].astype(BF16)


def _swa_prompt(sinks, q, kt, va, bb, batch, seq):
    nb = seq // WINDOW
    prev = lambda b, c: jnp.maximum(c - 1, 0)
    return pl.pallas_call(
        _swa_prompt_body,
        grid=(batch, nb),
        in_specs=[
            pl.BlockSpec(memory_space=pltpu.SMEM),
            pl.BlockSpec((WINDOW, QW), lambda b, c: (b * nb + c, 0)),
            pl.BlockSpec((1, KV_HEADS, HEAD_DIM, WINDOW), lambda b, c: (b, 0, 0, prev(b, c))),
            pl.BlockSpec((1, KV_HEADS, HEAD_DIM, WINDOW), lambda b, c: (b, 0, 0, c)),
            pl.BlockSpec((1, KV_HEADS, WINDOW, LANES), lambda b, c: (b, 0, prev(b, c), 0)),
            pl.BlockSpec((1, KV_HEADS, WINDOW, LANES), lambda b, c: (b, 0, c, 0)),
            _resident((N_HEADS, WINDOW, 2 * WINDOW), lambda b, c: (0, 0, 0)),
        ],
        out_specs=pl.BlockSpec((WINDOW, QW), lambda b, c: (b * nb + c, 0)),
        out_shape=jax.ShapeDtypeStruct((batch * seq, QW), BF16),
        compiler_params=_params("parallel", "arbitrary"),
        name="swa_prompt",
    )(sinks, q, kt, kt, va, va, bb)


def _swa_sample_body(sink_ref, q_ref, st_ref, new_ref, bias_ref, o_ref, *, tn):
    rows = GROUP * tn
    t = lax.broadcasted_iota(I32, (rows, 2 * WINDOW), 0) & (tn - 1)
    j = lax.broadcasted_iota(I32, (rows, 2 * WINDOW), 1)
    dist = t + WINDOW - j
    valid = (dist >= 0) & (dist <= WINDOW) & (j < WINDOW + tn)
    g_of_row = lax.broadcasted_iota(I32, (rows, 1), 0) >> (tn.bit_length() - 1)
    for b in range(q_ref.shape[0]):
        for n in range(KV_HEADS):
            kt = jnp.concatenate([st_ref[b, 0, n], new_ref[b, 0, n]], axis=1).astype(BF16)
            vt = jnp.concatenate([st_ref[b, 1, n], new_ref[b, 1, n]], axis=1).astype(BF16)
            s = jnp.zeros((rows, 1), F32)
            for g in range(GROUP):
                s = jnp.where(g_of_row == g, sink_ref[n * GROUP + g], s)
            lg = jnp.dot(q_ref[b, n], kt, preferred_element_type=F32)
            lg = jnp.where(valid, lg + bias_ref[n], NEG)
            m = jnp.maximum(jnp.max(lg, axis=1, keepdims=True), s)
            e = jnp.exp(lg - m)
            inv = 1.0 / (jnp.sum(e, axis=1, keepdims=True) + jnp.exp(s - m))
            p = (e * inv).astype(BF16)
            o = lax.dot_general(p, vt, (((1,), (1,)), ((), ())), preferred_element_type=F32)
            o_ref[b, n] = o.astype(BF16)


def _swa_sample(sinks, qs, st_t, new_t, bias, tn):
    nb = qs.shape[0]
    rows = GROUP * tn
    return pl.pallas_call(
        functools.partial(_swa_sample_body, tn=tn),
        grid=(nb // SWA_BATCH,),
        in_specs=[
            pl.BlockSpec(memory_space=pltpu.SMEM),
            pl.BlockSpec((SWA_BATCH, KV_HEADS, rows, HEAD_DIM), lambda b: (b, 0, 0, 0)),
            pl.BlockSpec((SWA_BATCH, 2, KV_HEADS, HEAD_DIM, WINDOW), lambda b: (b, 0, 0, 0, 0)),
            pl.BlockSpec((SWA_BATCH, 2, KV_HEADS, HEAD_DIM, WINDOW), lambda b: (b, 0, 0, 0, 0)),
            _resident((KV_HEADS, rows, 2 * WINDOW), lambda b: (0, 0, 0)),
        ],
        out_specs=pl.BlockSpec((SWA_BATCH, KV_HEADS, rows, HEAD_DIM), lambda b: (b, 0, 0, 0)),
        out_shape=jax.ShapeDtypeStruct((nb, KV_HEADS, rows, HEAD_DIM), BF16),
        compiler_params=_params("parallel"),
        name="swa_sample",
    )(sinks, qs, st_t, new_t, bias)


def _dsa_sample_body(pt_ref, qi_ref, w_ref, q_ref, new_idx_ref, new_kv_ref, bias_ref, tri_ref,
                     cidx_ref, ckv_ref, o_ref,
                     idx_buf, kv_buf, s_ref, mb_ref, idx_sem, kv_sem, *, tn, topk, n_pages):
    b = pl.program_id(0)
    nb = pl.num_programs(0)
    rows = GROUP * tn
    n_groups = n_pages // KV_GROUP
    past = n_pages * PAGE_SIZE
    slot = b % 2

    def idx_copy(bb, sl, p):
        return pltpu.make_async_copy(cidx_ref.at[0, pt_ref[bb, p]], idx_buf.at[sl, p], idx_sem.at[sl])

    def kv_copy(bb, grp, sl, p):
        return pltpu.make_async_copy(ckv_ref.at[0, pt_ref[bb, grp * KV_GROUP + p]], kv_buf.at[sl, p], kv_sem.at[sl])

    def start_idx(bb, sl):
        lax.fori_loop(0, n_pages, lambda p, _: (idx_copy(bb, sl, p).start(), 0)[1], 0)

    def start_kv(bb, grp, sl):
        for p in range(KV_GROUP):
            kv_copy(bb, grp, sl, p).start()

    @pl.when(b == 0)
    def _():
        start_idx(0, 0)
        start_kv(0, 0, 0)

    @pl.when(b + 1 < nb)
    def _():
        start_idx(b + 1, 1 - slot)

    lax.fori_loop(0, n_pages, lambda p, _: (idx_copy(b, slot, p).wait(), 0)[1], 0)

    qi = qi_ref[0]
    w = w_ref[0] * IDX_HEADS ** -0.5
    t_row = lax.broadcasted_iota(I32, (tn, LANES), 0)
    lane = lax.broadcasted_iota(I32, (tn, LANES), 1)

    def block_scores(kit):
        d = jnp.dot(qi, kit, preferred_element_type=F32)
        acc = jnp.zeros((tn, LANES), F32)
        for h in range(IDX_HEADS):
            acc = acc + jnp.maximum(d[h * tn:(h + 1) * tn], 0.0) * w[:, h:h + 1]
        return acc

    def score_page(p, _):
        s_ref[p] = block_scores(idx_buf[slot, p].astype(BF16))
        return 0

    lax.fori_loop(0, n_pages, score_page, 0)
    new_valid = (lane <= t_row) & (lane < tn)
    s_ref[n_pages] = jnp.where(new_valid, block_scores(new_idx_ref[0]), NEG)

    _write_mask_bias(s_ref, mb_ref, tri_ref, n_pages + 1, float(topk), 1,
                     lambda blk: (blk < n_pages) | new_valid)

    def update(carry, n, lg, vt):
        m, l, acc = carry
        m_new = jnp.maximum(m, jnp.max(lg, axis=1, keepdims=True))
        alpha = jnp.exp(m - m_new)
        p = jnp.exp(lg - m_new)
        l = alpha * l + jnp.sum(p, axis=1, keepdims=True)
        pv = lax.dot_general(p.astype(BF16), vt, (((1,), (1,)), ((), ())), preferred_element_type=F32)
        return m_new, l, alpha * acc + pv

    def group_step(grp, carry):
        sl = grp % 2
        for p in range(KV_GROUP):
            kv_copy(b, grp, sl, p).wait()

        @pl.when(grp + 1 < n_groups)
        def _():
            start_kv(b, grp + 1, 1 - sl)

        is_last = grp == n_groups - 1
        out = []
        for n in range(KV_HEADS):
            kt = jnp.concatenate([kv_buf[sl, p, 0, n] for p in range(KV_GROUP)], axis=1).astype(BF16)
            vt = jnp.concatenate([kv_buf[sl, p, 1, n] for p in range(KV_GROUP)], axis=1).astype(BF16)
            lg = jnp.dot(q_ref[0, n], kt, preferred_element_type=F32)
            mb = jnp.concatenate(
                [jnp.tile(mb_ref[grp * KV_GROUP + p], (GROUP, 1)) for p in range(KV_GROUP)], axis=1)
            near = jnp.where(is_last, bias_ref[n, :, :LANES], 0.0)
            pad = jnp.zeros((rows, (KV_GROUP - 1) * PAGE_SIZE), F32)
            lg = lg + mb + jnp.concatenate([pad, near], axis=1)
            out.append(update(carry[n], n, lg, vt))
        return tuple(out)

    init = tuple((jnp.full((rows, 1), NEG, F32), jnp.zeros((rows, 1), F32), jnp.zeros((rows, HEAD_DIM), F32))
                 for _ in range(KV_HEADS))
    carry = lax.fori_loop(0, n_groups, group_step, init)

    @pl.when(b + 1 < nb)
    def _():
        start_kv(b + 1, 0, 0)

    mb_new = jnp.tile(mb_ref[n_pages], (GROUP, 1))
    for n in range(KV_HEADS):
        kt = new_kv_ref[0, 0, n].astype(BF16)
        vt = new_kv_ref[0, 1, n].astype(BF16)
        lg = jnp.dot(q_ref[0, n], kt, preferred_element_type=F32) + mb_new + bias_ref[n, :, LANES:]
        _, l, acc = update(carry[n], n, lg, vt)
        o_ref[0, n] = (acc * (1.0 / l)).astype(BF16)


def _dsa_sample(page_table, qi_s, w_s, q_s, new_idx_t, new_kv_t, bias, tri, cidx_t, ckv_t, tn, topk):
    nb, n_pages = page_table.shape
    rows = GROUP * tn
    per_b = lambda *blk: pl.BlockSpec((1,) + blk, lambda b, pt: (b,) + (0,) * len(blk))
    return pl.pallas_call(
        functools.partial(_dsa_sample_body, tn=tn, topk=topk, n_pages=n_pages),
        grid_spec=pltpu.PrefetchScalarGridSpec(
            num_scalar_prefetch=1,
            grid=(nb,),
            in_specs=[
                per_b(IDX_HEADS * tn, IDX_DIM),
                per_b(tn, IDX_HEADS),
                per_b(KV_HEADS, rows, HEAD_DIM),
                per_b(IDX_DIM, LANES),
                per_b(2, KV_HEADS, HEAD_DIM, LANES),
                _resident((KV_HEADS, rows, 2 * LANES), lambda b, pt: (0, 0, 0)),
                _resident((LANES, 2 * LANES), lambda b, pt: (0, 0)),
                pl.BlockSpec(memory_space=pl.ANY),
                pl.BlockSpec(memory_space=pl.ANY),
            ],
            out_specs=per_b(KV_HEADS, rows, HEAD_DIM),
            scratch_shapes=[
                pltpu.VMEM((2, n_pages, IDX_DIM, PAGE_SIZE), F32),
                pltpu.VMEM((2, KV_GROUP, 2, KV_HEADS, HEAD_DIM, PAGE_SIZE), F32),
                pltpu.VMEM((n_pages + 1, tn, LANES), F32),
                pltpu.VMEM((n_pages + 1, tn, LANES), F32),
                pltpu.SemaphoreType.DMA((2,)),
                pltpu.SemaphoreType.DMA((2,)),
            ],
        ),
        out_shape=jax.ShapeDtypeStruct((nb, KV_HEADS, rows, HEAD_DIM), BF16),
        compiler_params=_params("arbitrary"),
        name="dsa_sample",
    )(page_table, qi_s, w_s, q_s, new_idx_t, new_kv_t, bias, tri, cidx_t, ckv_t)


def _t5_bucket(dist):
    n = jnp.maximum(dist, 0)
    max_exact = N_BUCKETS // 2
    nf = jnp.maximum(n, 1).astype(F32)
    log_part = jnp.log(nf / max_exact) / math.log(MAX_DISTANCE / max_exact) * (N_BUCKETS - max_exact)
    large = jnp.minimum(max_exact + log_part.astype(I32), N_BUCKETS - 1)
    return jnp.where(n < max_exact, n, large)


def _band_bias(table):
    r = jnp.arange(WINDOW, dtype=I32)[:, None]
    j = jnp.arange(2 * WINDOW, dtype=I32)[None, :]
    return jnp.transpose(table[_t5_bucket(r + WINDOW - j)].astype(F32), (2, 0, 1))


def _tie_prefix_matrix():
    a = jnp.arange(LANES)[:, None]
    b = jnp.arange(2 * LANES)[None, :]
    return ((a <= b) | (b >= LANES)).astype(BF16)


def _heads_to_rows(q, nb, tn):
    q = q.reshape(nb, tn, KV_HEADS, GROUP, HEAD_DIM)
    return jnp.transpose(q, (0, 2, 3, 1, 4)).reshape(nb, KV_HEADS, GROUP * tn, HEAD_DIM)


def _rows_to_heads(o, nb, tn):
    o = o.reshape(nb, KV_HEADS, GROUP, tn, HEAD_DIM)
    return jnp.transpose(o, (0, 3, 1, 2, 4)).reshape(nb * tn, QW)


def _prompt_kv_layouts(kv, batch, seq, chunk):
    kvb = kv.astype(BF16).reshape(batch, seq, 2, KV_HEADS, HEAD_DIM)
    k = kvb[:, :, 0].reshape(batch, seq // chunk, chunk, KV_HEADS, HEAD_DIM)
    kt = jnp.transpose(k, (0, 3, 1, 4, 2))
    v = jnp.transpose(kvb[:, :, 1], (0, 2, 1, 3))
    va = jnp.concatenate([v, jnp.ones_like(v)], axis=-1)
    return kt, va


def _new_tokens_t(kv, nb, tn):
    t = jnp.transpose(kv.reshape(nb, tn, 2, KV_HEADS, HEAD_DIM), (0, 2, 3, 4, 1))
    return jnp.pad(t, ((0, 0),) * 4 + ((0, LANES - tn),))


def kernel(x_prompt, x_sample, cache_kv, cache_idx, state_win_kv, page_table, rel_bias, ffn_norm, ffn_w_gate,
           ffn_w_up, ffn_w_down, mix_norm, a_w_in, a_w_out, b_w_in, b_w_out, b_sinks, final_norm):
    batch, seq, d = x_prompt.shape
    nb, tn, _ = x_sample.shape
    n_prompt = batch * seq
    n_pages = page_table.shape[1]
    depth = ffn_norm.shape[0]
    assert tn & (tn - 1) == 0, "sample kernels index rows as (head, token) with shifts"

    x = jnp.concatenate([x_prompt.reshape(n_prompt, d), x_sample.reshape(nb * tn, d)], axis=0)
    wg, wu, wd = ffn_w_gate.astype(BF16), ffn_w_up.astype(BF16), ffn_w_down.astype(BF16)
    gf = final_norm.reshape(1, d)
    ffn_norm = ffn_norm.reshape(2 * depth, 1, d)
    mix_norm = mix_norm.reshape(depth, 1, d)

    bb = _band_bias(rel_bias)
    bb_rel = bb - rel_bias[N_BUCKETS - 1].astype(F32)[:, None, None]
    tri = _tie_prefix_matrix()

    def sample_bias(t):
        return t[:, :tn, :].reshape(KV_HEADS, GROUP * tn, 2 * WINDOW)

    cidx_t = jnp.transpose(cache_idx, (0, 1, 3, 2))
    ckv_t = jnp.transpose(cache_kv, (0, 1, 3, 4, 5, 2))
    st_t = jnp.transpose(state_win_kv, (0, 1, 3, 4, 5, 2))

    outs = {k: [] for k in ("kv_p", "idx_p", "win_p", "kv_s", "idx_s", "win_s")}
    for i in range(depth):
        x = _ffn(x, ffn_norm, wg, wu, wd, i, 0, gf, False)
        j = i // 2
        if i % 2 == 0:
            w_in = jnp.pad(a_w_in[j], ((0, 0), (0, A_PAD - A_USED))).astype(BF16)
            q, kv, qi, kiw = _proj_a(x, mix_norm, w_in, i)
            ki = kiw[:, :IDX_DIM]
            kt, va = _prompt_kv_layouts(kv[:n_prompt], batch, seq, KEY_CHUNK)
            kit = jnp.transpose(ki[:n_prompt].astype(BF16).reshape(batch, seq // KEY_CHUNK, KEY_CHUNK, IDX_DIM),
                                (0, 1, 3, 2))
            o_p = _dsa_prompt(q, qi, kiw, kt, va, kit, bb_rel, tri,
                              batch, seq, min(TOPK_MAX, seq // 4))

            qi_s = jnp.transpose(qi[n_prompt:].reshape(nb, tn, IDX_HEADS, IDX_DIM), (0, 2, 1, 3))
            qi_s = qi_s.reshape(nb, IDX_HEADS * tn, IDX_DIM)
            w_s = kiw[n_prompt:, IDX_DIM:IDX_DIM + IDX_HEADS].reshape(nb, tn, IDX_HEADS)
            new_idx_t = jnp.pad(jnp.transpose(ki[n_prompt:].reshape(nb, tn, IDX_DIM), (0, 2, 1)),
                                ((0, 0), (0, 0), (0, LANES - tn))).astype(BF16)
            o_s = _dsa_sample(page_table, qi_s, w_s, _heads_to_rows(q[n_prompt:], nb, tn), new_idx_t,
                              _new_tokens_t(kv[n_prompt:], nb, tn), sample_bias(bb_rel), tri,
                              cidx_t[j:j + 1], ckv_t[j:j + 1], tn,
                              min(TOPK_MAX, (n_pages * PAGE_SIZE + tn) // 4))
            w_out = a_w_out[j]
            outs["kv_p"].append(kv[:n_prompt].reshape(batch, seq, 2, KV_HEADS, HEAD_DIM))
            outs["idx_p"].append(ki[:n_prompt].reshape(batch, seq, IDX_DIM))
            outs["kv_s"].append(kv[n_prompt:].reshape(nb, tn, 2, KV_HEADS, HEAD_DIM))
            outs["idx_s"].append(ki[n_prompt:].reshape(nb, tn, IDX_DIM))
        else:
            q, kv = _proj_b(x, mix_norm, b_w_in[j].astype(BF16), i)
            kt, va = _prompt_kv_layouts(kv[:n_prompt], batch, seq, seq)
            o_p = _swa_prompt(b_sinks[j], q, kt[:, :, 0], va, bb, batch, seq)
            new_t = _new_tokens_t(kv[n_prompt:], nb, tn)
            o_s = _swa_sample(b_sinks[j], _heads_to_rows(q[n_prompt:], nb, tn), st_t[j], new_t, sample_bias(bb), tn)
            w_out = b_w_out[j]
            kv_p = kv[:n_prompt].reshape(batch, seq, 2, KV_HEADS, HEAD_DIM)
            outs["win_p"].append(kv_p[:, seq - WINDOW:])
            win_t = jnp.concatenate([st_t[j][..., tn:], new_t[..., :tn]], axis=-1)
            outs["win_s"].append(jnp.transpose(win_t, (0, 4, 1, 2, 3)))
        o = jnp.concatenate([o_p, _rows_to_heads(o_s, nb, tn)], axis=0)
        x = _proj_out(x, o, w_out.astype(BF16))
        x = _ffn(x, ffn_norm, wg, wu, wd, i, 1, gf, i == depth - 1)

    return (x[:n_prompt].reshape(batch, seq, d), x[n_prompt:].reshape(nb, tn, d),
            jnp.stack(outs["kv_p"]), jnp.stack(outs["idx_p"]), jnp.stack(outs["win_p"]),
            jnp.stack(outs["kv_s"]), jnp.stack(outs["idx_s"]), jnp.stack(outs["win_s"]))
```
